```python
import jax, jax.numpy as jnp
from jax import lax
import numpy as np

D_MODEL = 2048
BATCH = 2
SEQ = 16384
DEPTH = 2

CHUNK = 64
FOX_HEADS = 6
FOX_HEAD_DIM = 128
FOX_WIDTH = FOX_HEADS * FOX_HEAD_DIM
FOX_Q_BLOCK = 128
SG_GROUPS = 4
SG_GROUP_DIM = 128
SG_WIDTH = SG_GROUPS * SG_GROUP_DIM
SG_LEN = 128
RET_HEADS = 6
RET_KEY_DIM = 128
RET_VAL_DIM = 128
RET_WIDTH = RET_HEADS * RET_VAL_DIM
ROPE_BASE = 10000.0

N_BRANCH = 3
MIX_WIDTH = FOX_WIDTH + SG_WIDTH + RET_WIDTH
D_FF = ((8 * D_MODEL + 3 * 256 - 1) // (3 * 256)) * 256
ALPHA = (2 * DEPTH) ** 0.25
BETA = (8 * DEPTH) ** -0.25
LN_EPS = 1e-5

IN_SIZES = (FOX_WIDTH, FOX_WIDTH, FOX_WIDTH, FOX_HEADS, 2 * SG_WIDTH,
            RET_HEADS * RET_KEY_DIM, RET_HEADS * RET_KEY_DIM, RET_WIDTH, RET_WIDTH,
            N_BRANCH * D_MODEL)
IN_OFFSETS = tuple(int(o) for o in np.cumsum(IN_SIZES)[:-1])
N_IN = int(sum(IN_SIZES))
FORGET_OFFSET = 3 * FOX_WIDTH
BRANCH_OFFSETS = (FOX_WIDTH, FOX_WIDTH + SG_WIDTH)

kernel_name = 'chunk_causal_hybrid_fox_gmlp_retnet_block'


def layer_norm(x, g=None, b=None):
    x32 = x.astype(jnp.float32)
    mu = jnp.mean(x32, axis=-1, keepdims=True)
    var = jnp.mean(jnp.square(x32 - mu), axis=-1, keepdims=True)
    y = (x32 - mu) * lax.rsqrt(var + LN_EPS)
    if g is not None:
        y = y * g.astype(jnp.float32) + b.astype(jnp.float32)
    return y.astype(x.dtype)


def modulate(x, shift, scale):
    return layer_norm(x) * (1 + scale[:, None, :]) + shift[:, None, :]


def forgetting_attention(q, k, v, f_logit):
    b, s, _ = q.shape
    q = q.reshape(b, s, FOX_HEADS, FOX_HEAD_DIM).transpose(0, 2, 1, 3)
    k = k.reshape(b, s, FOX_HEADS, FOX_HEAD_DIM).transpose(0, 2, 1, 3)
    v = v.reshape(b, s, FOX_HEADS, FOX_HEAD_DIM).transpose(0, 2, 1, 3)
    log_f = jax.nn.log_sigmoid(f_logit.astype(jnp.float32))
    cum_f = jnp.cumsum(log_f, axis=1).transpose(0, 2, 1)
    nb = s // FOX_Q_BLOCK
    qb = q.reshape(b, FOX_HEADS, nb, FOX_Q_BLOCK, FOX_HEAD_DIM).transpose(2, 0, 1, 3, 4)
    fq = cum_f.reshape(b, FOX_HEADS, nb, FOX_Q_BLOCK).transpose(2, 0, 1, 3)
    qpos = jnp.arange(s).reshape(nb, FOX_Q_BLOCK)
    kpos = jnp.arange(s)
    scale = FOX_HEAD_DIM ** -0.5

    def block(args):
        qi, fi, pi = args
        logits = (jnp.einsum('bhqd,bhkd->bhqk', qi, k).astype(jnp.float32) * scale
                  + (fi[..., None] - cum_f[:, :, None, :]))
        logits = jnp.where(kpos[None, :] <= pi[:, None], logits, -jnp.inf)
        p = jax.nn.softmax(logits, axis=-1)
        return jnp.einsum('bhqk,bhkd->bhqd', p.astype(v.dtype), v)

    o = lax.map(block, (qb, fq, qpos))
    return o.transpose(1, 0, 3, 2, 4).reshape(b, s, FOX_WIDTH)


def spatial_gating(z, ln_g, ln_b, w_s, b_s):
    b, s, _ = z.shape
    u, v = jnp.split(z, 2, axis=-1)
    v = layer_norm(v, ln_g, ln_b)
    v = v.reshape(b, s // SG_LEN, SG_LEN, SG_GROUPS, SG_GROUP_DIM)
    blk = jnp.arange(SG_LEN) // CHUNK
    mask = blk[:, None] >= blk[None, :]
    w = jnp.where(mask[None], w_s, 0).astype(v.dtype)
    vm = jnp.einsum('gts,bnsgc->bntgc', w, v) + b_s.T[:, :, None].astype(v.dtype)
    return u * vm.reshape(b, s, SG_WIDTH)


def rotary(x, cos, sin):
    half = x.shape[-1] // 2
    x1, x2 = x[..., :half], x[..., half:]
    return jnp.concatenate([x1 * cos - x2 * sin, x1 * sin + x2 * cos], axis=-1)


def retention(q, k, v, g, gn_g, gn_b):
    b, s, _ = q.shape
    dt = q.dtype
    f32 = jnp.float32
    q = q.astype(f32).reshape(b, s, RET_HEADS, RET_KEY_DIM)
    k = k.astype(f32).reshape(b, s, RET_HEADS, RET_KEY_DIM)
    v = v.astype(f32).reshape(b, s, RET_HEADS, RET_VAL_DIM)
    pos = jnp.arange(s, dtype=f32)
    inv_freq = ROPE_BASE ** (-jnp.arange(0, RET_KEY_DIM, 2, dtype=f32) / RET_KEY_DIM)
    ang = pos[:, None] * inv_freq[None, :]
    cos, sin = jnp.cos(ang)[:, None, :], jnp.sin(ang)[:, None, :]
    q = rotary(q, cos, sin) * (RET_KEY_DIM ** -0.5)
    k = rotary(k, cos, sin)
    log_gamma = jnp.log(1.0 - 2.0 ** (-5.0 - jnp.arange(RET_HEADS, dtype=f32)))
    n = s // CHUNK
    qc = q.reshape(b, n, CHUNK, RET_HEADS, RET_KEY_DIM)
    kc = k.reshape(b, n, CHUNK, RET_HEADS, RET_KEY_DIM)
    vc = v.reshape(b, n, CHUNK, RET_HEADS, RET_VAL_DIM)
    idx = jnp.arange(CHUNK, dtype=f32)
    intra_decay = jnp.exp(jnp.abs(idx[:, None] - idx[None, :])[None] * log_gamma[:, None, None])
    scores = jnp.einsum('bnihd,bnjhd->bnhij', qc, kc) * intra_decay
    o_intra = jnp.einsum('bnhij,bnjhe->bnihe', scores, vc)
    kv = jnp.einsum('bnjhd,bnjhe,hj->nbhde', kc, vc,
                    jnp.exp((CHUNK - 1 - idx)[None, :] * log_gamma[:, None]))
    chunk_decay = jnp.exp(CHUNK * log_gamma)[None, :, None, None]

    def step(state, kv_n):
        return state * chunk_decay + kv_n, state

    _, r_prev = lax.scan(step, jnp.zeros((b, RET_HEADS, RET_KEY_DIM, RET_VAL_DIM), f32), kv)
    o_cross = jnp.einsum('bnihd,nbhde,hi->bnihe', qc, r_prev,
                         jnp.exp((idx + 1)[None, :] * log_gamma[:, None]))
    o = (o_intra + o_cross).reshape(b, s, RET_HEADS, RET_VAL_DIM)
    o = layer_norm(o, gn_g.reshape(RET_HEADS, RET_VAL_DIM), gn_b.reshape(RET_HEADS, RET_VAL_DIM))
    return (jax.nn.silu(g.astype(f32)) * o.reshape(b, s, RET_WIDTH)).astype(dt)


def hybrid_layer(x, c, w_ada, b_ada, w_in, b_in, sg_ln_g, sg_ln_b, sg_w, sg_b,
                 ret_gn_g, ret_gn_b, w_branch, w_out, ln1_g, ln1_b,
                 w_ffn_in, w_ffn_out, ln2_g, ln2_b):
    b, s, _ = x.shape
    mod = jax.nn.silu(c) @ w_ada + b_ada
    shift1, scale1, gate1, shift2, scale2, gate2 = jnp.split(mod, 6, axis=-1)

    h = modulate(x, shift1, scale1)
    proj = h @ w_in + b_in
    (fq, fk, fv, f_logit, z_sg, rq, rk, rv, rg, merge_logits) = jnp.split(proj, IN_OFFSETS, axis=-1)
    o_fox = forgetting_attention(fq, fk, fv, f_logit)
    o_sg = spatial_gating(jax.nn.gelu(z_sg), sg_ln_g, sg_ln_b, sg_w, sg_b)
    o_ret = retention(rq, rk, rv, rg, ret_gn_g, ret_gn_b)
    w_fox, w_sg, w_ret = jnp.split(w_branch, BRANCH_OFFSETS, axis=0)
    gates = jax.nn.sigmoid(merge_logits).reshape(b, s, N_BRANCH, D_MODEL)
    merged = (gates[:, :, 0] * (o_fox @ w_fox)
              + gates[:, :, 1] * (o_sg @ w_sg)
              + gates[:, :, 2] * (o_ret @ w_ret))
    mix = merged @ w_out
    x = layer_norm(ALPHA * x + (1 + gate1[:, None, :]) * mix, ln1_g, ln1_b)

    h2 = modulate(x, shift2, scale2)
    a, u = jnp.split(h2 @ w_ffn_in, 2, axis=-1)
    ffn = (jax.nn.silu(a) * u) @ w_ffn_out
    x = layer_norm(ALPHA * x + (1 + gate2[:, None, :]) * ffn, ln2_g, ln2_b)
    return x


def setup_inputs(seed: int = 0) -> dict:
    key = jax.random.key(seed)
    ks = jax.random.split(key, 24)
    f32 = jnp.float32
    nrm = lambda k, shape: jax.random.normal(k, shape, f32)
    col_scale = np.concatenate([np.full(sz, BETA if i in (2, 7) else 1.0, np.float32)
                                for i, sz in enumerate(IN_SIZES)])
    row_scale = np.concatenate([np.full(FOX_WIDTH, FOX_WIDTH ** -0.5, np.float32),
                                np.full(SG_WIDTH, SG_WIDTH ** -0.5, np.float32),
                                np.full(RET_WIDTH, RET_WIDTH ** -0.5, np.float32)])
    b_in = 0.01 * nrm(ks[5], (DEPTH, N_IN))
    b_in = b_in.at[:, FORGET_OFFSET:FORGET_OFFSET + FOX_HEADS].set(
        2.0 + 3.0 * jax.random.uniform(ks[6], (DEPTH, FOX_HEADS), f32))
    return {
        'x': nrm(ks[0], (BATCH, SEQ, D_MODEL)),
        'c': nrm(ks[1], (BATCH, D_MODEL)),
        'w_ada': nrm(ks[2], (DEPTH, D_MODEL, 6 * D_MODEL)) * (0.1 * D_MODEL ** -0.5),
        'b_ada': 0.01 * nrm(ks[3], (DEPTH, 6 * D_MODEL)),
        'w_in': nrm(ks[4], (DEPTH, D_MODEL, N_IN)) * (D_MODEL ** -0.5) * jnp.asarray(col_scale),
        'b_in': b_in,
        'sg_ln_g': 1.0 + 0.01 * nrm(ks[7], (DEPTH, SG_WIDTH)),
        'sg_ln_b': 0.01 * nrm(ks[8], (DEPTH, SG_WIDTH)),
        'sg_w': nrm(ks[9], (DEPTH, SG_GROUPS, SG_LEN, SG_LEN)) * (SG_LEN ** -0.5),
        'sg_b': 1.0 + 0.01 * nrm(ks[10], (DEPTH, SG_GROUPS, SG_LEN)),
        'ret_gn_g': 1.0 + 0.01 * nrm(ks[11], (DEPTH, RET_WIDTH)),
        'ret_gn_b': 0.01 * nrm(ks[12], (DEPTH, RET_WIDTH)),
        'w_branch': nrm(ks[13], (DEPTH, MIX_WIDTH, D_MODEL)) * jnp.asarray(row_scale)[None, :, None],
        'w_out': nrm(ks[14], (DEPTH, D_MODEL, D_MODEL)) * (BETA * D_MODEL ** -0.5),
        'ln1_g': 1.0 + 0.01 * nrm(ks[15], (DEPTH, D_MODEL)),
        'ln1_b': 0.01 * nrm(ks[16], (DEPTH, D_MODEL)),
        'w_ffn_in': nrm(ks[17], (DEPTH, D_MODEL, 2 * D_FF)) * (D_MODEL ** -0.5),
        'w_ffn_out': nrm(ks[18], (DEPTH, D_FF, D_MODEL)) * (BETA * D_FF ** -0.5),
        'ln2_g': 1.0 + 0.01 * nrm(ks[19], (DEPTH, D_MODEL)),
        'ln2_b': 0.01 * nrm(ks[20], (DEPTH, D_MODEL)),
    }


def reference(x, c, w_ada, b_ada, w_in, b_in, sg_ln_g, sg_ln_b, sg_w, sg_b,
              ret_gn_g, ret_gn_b, w_branch, w_out, ln1_g, ln1_b,
              w_ffn_in, w_ffn_out, ln2_g, ln2_b):
    for l in range(DEPTH):
        x = hybrid_layer(x, c, w_ada[l], b_ada[l], w_in[l], b_in[l], sg_ln_g[l], sg_ln_b[l],
                         sg_w[l], sg_b[l], ret_gn_g[l], ret_gn_b[l], w_branch[l], w_out[l],
                         ln1_g[l], ln1_b[l], w_ffn_in[l], w_ffn_out[l], ln2_g[l], ln2_b[l])
    return x
```

```python
import functools

import numpy as np
import jax
import jax.numpy as jnp
from jax import lax
from jax.experimental import pallas as pl
from jax.experimental.pallas import tpu as pltpu

F32 = jnp.float32
BF16 = jnp.bfloat16

CHUNK = 64
FOX_HEADS = 6
HEAD_DIM = 128
FOX_WIDTH = FOX_HEADS * HEAD_DIM
SG_GROUPS = 4
SG_LEN = 128
SG_WIDTH = SG_GROUPS * 128
RET_HEADS = 6
RET_WIDTH = RET_HEADS * HEAD_DIM
ROPE_BASE = 10000.0
N_BRANCH = 3
LN_EPS = 1e-5
LANES = 128

VMEM_LIMIT_BYTES = 56 * 1024 * 1024


def _params(*semantics):
    return pltpu.CompilerParams(dimension_semantics=semantics,
                                vmem_limit_bytes=VMEM_LIMIT_BYTES)


def _dot(a, b):
    return jnp.dot(a, b, preferred_element_type=F32)


def _dot_nt(a, b):
    return lax.dot_general(a, b, (((1,), (1,)), ((), ())), preferred_element_type=F32)


def _dot_tn(a, b):
    return lax.dot_general(a, b, (((0,), (0,)), ((), ())), preferred_element_type=F32)


def _split2(a):
    hi = a.astype(BF16)
    lo = (a - hi.astype(F32)).astype(BF16)
    return hi, lo


def _split3(a):
    hi = a.astype(BF16)
    r = a - hi.astype(F32)
    mid = r.astype(BF16)
    lo = (r - mid.astype(F32)).astype(BF16)
    return hi, mid, lo


def _layer_norm_rows(x):
    mu = jnp.mean(x, axis=-1, keepdims=True)
    xc = x - mu
    var = jnp.mean(xc * xc, axis=-1, keepdims=True)
    return xc * lax.rsqrt(var + LN_EPS)


def _const_spec(shape):
    return pl.BlockSpec(shape, lambda *_: (0,) * len(shape), pipeline_mode=pl.Buffered(1))


def _ada_kernel(c_ref, w_ref, b_ref, o_ref):
    c = c_ref[...]
    a = c * jax.nn.sigmoid(c)
    a_hi, a_lo = _split2(a)
    w_hi, w_lo = _split2(w_ref[0])
    o_ref[0] = _dot(a_hi, w_hi) + _dot(a_lo, w_hi) + _dot(a_hi, w_lo) + b_ref[0]


def _ada(c, w_ada, b_ada):
    depth, d, n = w_ada.shape
    b = c.shape[0]
    rows = 8
    tn = 1024
    c_pad = jnp.zeros((rows, d), F32).at[:b].set(c)
    out = pl.pallas_call(
        _ada_kernel,
        grid=(depth, n // tn),
        in_specs=[pl.BlockSpec((rows, d), lambda l, j: (0, 0)),
                  pl.BlockSpec((1, d, tn), lambda l, j: (l, 0, j)),
                  pl.BlockSpec((1, 1, tn), lambda l, j: (l, 0, j))],
        out_specs=pl.BlockSpec((1, rows, tn), lambda l, j: (l, 0, j)),
        out_shape=jax.ShapeDtypeStruct((depth, rows, n), F32),
        compiler_params=_params("parallel", "parallel"),
        name="ada",
    )(c_pad, w_ada, b_ada.reshape(depth, 1, n))
    return out[:, :b]


def _inproj_kernel(x_ref, sh_ref, sc_ref, w_ref, b_ref, wf_ref, bf_ref, o_ref, fl_ref, h_ref):
    @pl.when(pl.program_id(1) == 0)
    def _():
        h = _layer_norm_rows(x_ref[...]) * (1.0 + sc_ref[0]) + sh_ref[0]
        h_hi, h_lo = _split2(h)
        h_ref[...] = h_hi
        wf_hi, wf_lo = _split2(wf_ref[...])
        fl_ref[...] = (_dot(h_hi, wf_hi) + _dot(h_lo, wf_hi) + _dot(h_hi, wf_lo)
                       + bf_ref[...])

    o_ref[...] = (_dot(h_ref[...], w_ref[...]) + b_ref[...]).astype(o_ref.dtype)


def _inproj(x2d, shift, scale, w_main, b_main, w_f, b_f, seq, tm, tn):
    m, d = x2d.shape
    n = w_main.shape[1]
    per_batch = seq // tm
    return pl.pallas_call(
        _inproj_kernel,
        grid=(m // tm, n // tn),
        in_specs=[pl.BlockSpec((tm, d), lambda i, j: (i, 0)),
                  pl.BlockSpec((1, 1, d), lambda i, j: (i // per_batch, 0, 0)),
                  pl.BlockSpec((1, 1, d), lambda i, j: (i // per_batch, 0, 0)),
                  pl.BlockSpec((d, tn), lambda i, j: (0, j)),
                  pl.BlockSpec((1, tn), lambda i, j: (0, j)),
                  pl.BlockSpec((d, LANES), lambda i, j: (0, 0)),
                  pl.BlockSpec((1, LANES), lambda i, j: (0, 0))],
        out_specs=[pl.BlockSpec((tm, tn), lambda i, j: (i, j)),
                   pl.BlockSpec((tm, LANES), lambda i, j: (i, 0))],
        out_shape=[jax.ShapeDtypeStruct((m, n), BF16),
                   jax.ShapeDtypeStruct((m, LANES), F32)],
        scratch_shapes=[pltpu.VMEM((tm, d), BF16)],
        compiler_params=_params("parallel", "arbitrary"),
        name="inproj",
    )(x2d, shift, scale, w_main, b_main, w_f, b_f)


def _cumf_kernel(f_ref, o_ref):
    f = f_ref[0]
    rows = f.shape[0]
    logf = jnp.minimum(f, 0.0) - jnp.log1p(jnp.exp(-jnp.abs(f)))
    r_i = lax.broadcasted_iota(jnp.int32, (LANES, LANES), 0)
    c_i = lax.broadcasted_iota(jnp.int32, (LANES, LANES), 1)
    upper = (r_i <= c_i).astype(BF16)
    hi, mid, lo = _split3(logf)
    within = _dot(hi, upper) + _dot(mid, upper) + _dot(lo, upper)
    tot = jnp.broadcast_to(within[:, LANES - 1:LANES], (rows, LANES))
    rr = lax.broadcasted_iota(jnp.int32, (rows, rows), 0)
    cc = lax.broadcasted_iota(jnp.int32, (rows, rows), 1)
    strict = (cc < rr).astype(BF16)
    t_hi, t_mid, t_lo = _split3(tot)
    before = _dot(strict, t_hi) + _dot(strict, t_mid) + _dot(strict, t_lo)
    o_ref[0] = within + before


def _cumf(f_rows):
    n, rows, _ = f_rows.shape
    return pl.pallas_call(
        _cumf_kernel,
        grid=(n,),
        in_specs=[pl.BlockSpec((1, rows, LANES), lambda i: (i, 0, 0))],
        out_specs=pl.BlockSpec((1, rows, LANES), lambda i: (i, 0, 0)),
        out_shape=jax.ShapeDtypeStruct((n, rows, LANES), F32),
        compiler_params=_params("parallel"),
        name="cumf",
    )(f_rows)


def _fox_kernel(q_ref, k_ref, v_ref, cfq_ref, cfk_ref, o_ref, m_ref, l_ref, acc_ref, *, tq, tk):
    i = pl.program_id(2)
    q = q_ref[...]
    cfq = cfq_ref[0]
    m_ref[...] = jnp.full(m_ref.shape, -jnp.inf, F32)
    l_ref[...] = jnp.zeros(l_ref.shape, F32)
    acc_ref[...] = jnp.zeros(acc_ref.shape, F32)

    def step(j, masked):
        ks = pl.multiple_of(j * tk, tk)
        k = k_ref[pl.ds(ks, tk), :]
        v = v_ref[pl.ds(ks, tk), :]
        s = _dot_nt(q, k) + (cfq - cfk_ref[0, :, pl.ds(ks, tk)])
        if masked:
            qpos = i * tq + lax.broadcasted_iota(jnp.int32, (tq, tk), 0)
            kpos = ks + lax.broadcasted_iota(jnp.int32, (tq, tk), 1)
            s = jnp.where(kpos <= qpos, s, -jnp.inf)
        m_prev = m_ref[...]
        m_new = jnp.maximum(m_prev, jnp.max(s, axis=-1, keepdims=True))
        alpha = jnp.exp(m_prev - m_new)
        p = jnp.exp(s - m_new)
        l_ref[...] = alpha * l_ref[...] + jnp.sum(p, axis=-1, keepdims=True)
        acc_ref[...] = alpha * acc_ref[...] + _dot(p.astype(BF16), v)
        m_ref[...] = m_new

    n_diag = tq // tk
    n_full = i * n_diag

    def body(j, carry):
        step(j, False)
        return carry

    lax.fori_loop(0, n_full, body, 0)
    for d in range(n_diag):
        step(n_full + d, True)
    o_ref[...] = (acc_ref[...] / l_ref[...]).astype(o_ref.dtype)


def _fox(proj, cf, batch, seq, col0, tq, tk):
    m = proj.shape[0]
    nq = seq // tq
    cf_row = cf.reshape(batch * FOX_HEADS, 1, seq)
    cf_col = cf.reshape(batch * FOX_HEADS, seq, 1)
    cb = col0 // HEAD_DIM
    return pl.pallas_call(
        functools.partial(_fox_kernel, tq=tq, tk=tk),
        grid=(batch, FOX_HEADS, nq),
        in_specs=[pl.BlockSpec((tq, HEAD_DIM), lambda b, h, i: (b * nq + i, cb + h)),
                  pl.BlockSpec((seq, HEAD_DIM), lambda b, h, i: (b, cb + FOX_HEADS + h)),
                  pl.BlockSpec((seq, HEAD_DIM), lambda b, h, i: (b, cb + 2 * FOX_HEADS + h)),
                  pl.BlockSpec((1, tq, 1), lambda b, h, i: (b * FOX_HEADS + h, i, 0)),
                  pl.BlockSpec((1, 1, seq), lambda b, h, i: (b * FOX_HEADS + h, 0, 0))],
        out_specs=pl.BlockSpec((tq, HEAD_DIM), lambda b, h, i: (b * nq + i, h)),
        out_shape=jax.ShapeDtypeStruct((m, FOX_WIDTH), BF16),
        scratch_shapes=[pltpu.VMEM((tq, 1), F32), pltpu.VMEM((tq, 1), F32),
                        pltpu.VMEM((tq, HEAD_DIM), F32)],
        compiler_params=_params("parallel", "parallel", "arbitrary"),
        name="fox",
    )(proj, proj, proj, cf_col, cf_row)


def _gelu_tanh(x):
    return 0.5 * x * (1.0 + jnp.tanh(0.7978845608028654 * (x + 0.044715 * (x * x * x))))


def _sg_kernel(u_ref, v_ref, g_ref, b_ref, w_ref, bs_ref, o_ref, *, tr):
    u = _gelu_tanh(u_ref[...].astype(F32))
    v = _gelu_tanh(v_ref[...].astype(F32))
    v = (_layer_norm_rows(v) * g_ref[...] + b_ref[...]).astype(BF16)
    blk = lax.broadcasted_iota(jnp.int32, (SG_LEN, SG_LEN), 0) // CHUNK
    blk_t = lax.broadcasted_iota(jnp.int32, (SG_LEN, SG_LEN), 1) // CHUNK
    keep = blk >= blk_t
    for g in range(SG_GROUPS):
        w = jnp.where(keep, w_ref[g], 0.0).astype(BF16)
        bias = bs_ref[:, g:g + 1]
        cols = slice(g * 128, (g + 1) * 128)
        for p in range(tr // SG_LEN):
            rows = slice(p * SG_LEN, (p + 1) * SG_LEN)
            vm = _dot(w, v[rows, cols]) + bias
            o_ref[rows, cols] = (u[rows, cols] * vm).astype(o_ref.dtype)


def _sg(proj, ln_g, ln_b, w_s, b_s, col0, tr):
    m = proj.shape[0]
    cb = col0 // SG_WIDTH
    return pl.pallas_call(
        functools.partial(_sg_kernel, tr=tr),
        grid=(m // tr,),
        in_specs=[pl.BlockSpec((tr, SG_WIDTH), lambda i: (i, cb)),
                  pl.BlockSpec((tr, SG_WIDTH), lambda i: (i, cb + 1)),
                  pl.BlockSpec((1, SG_WIDTH), lambda i: (0, 0)),
                  pl.BlockSpec((1, SG_WIDTH), lambda i: (0, 0)),
                  pl.BlockSpec((SG_GROUPS, SG_LEN, SG_LEN), lambda i: (0, 0, 0)),
                  pl.BlockSpec((SG_LEN, SG_GROUPS), lambda i: (0, 0))],
        out_specs=pl.BlockSpec((tr, SG_WIDTH), lambda i: (i, 0)),
        out_shape=jax.ShapeDtypeStruct((m, SG_WIDTH), BF16),
        compiler_params=_params("parallel"),
        name="sg",
    )(proj, proj, ln_g.reshape(1, -1), ln_b.reshape(1, -1), w_s, b_s.T)


def _ret_tables(seq, t):
    half = HEAD_DIM // 2
    inv_freq = ROPE_BASE ** (-np.arange(0, HEAD_DIM, 2, dtype=np.float64) / HEAD_DIM)
    ang = np.arange(seq, dtype=np.float64)[:, None] * inv_freq[None, :]
    cos2 = np.concatenate([np.cos(ang), np.cos(ang)], axis=-1)
    sin2 = np.concatenate([-np.sin(ang), np.sin(ang)], axis=-1)
    log_gamma = np.log(1.0 - 2.0 ** (-5.0 - np.arange(RET_HEADS, dtype=np.float64)))
    pos = np.arange(t, dtype=np.float64)
    same_or_earlier = (pos[None, :] // CHUNK) <= (pos[:, None] // CHUNK)
    dist = np.abs(pos[:, None] - pos[None, :])
    dmat = np.where(same_or_earlier[None], np.exp(dist[None] * log_gamma[:, None, None]), 0.0)
    dq = np.exp((pos + 1.0)[None, :] * log_gamma[:, None])
    dk = np.exp((t - 1.0 - pos)[None, :] * log_gamma[:, None])
    dq = np.repeat(dq.T[:, :, None], HEAD_DIM, axis=2).reshape(t, RET_WIDTH)
    dk = np.repeat(dk.T[:, :, None], HEAD_DIM, axis=2).reshape(t, RET_WIDTH)
    dstate = np.exp(t * log_gamma)
    assert half * 2 == HEAD_DIM
    f = lambda a: jnp.asarray(a, F32)
    return f(cos2), f(sin2), f(dmat), f(dq), f(dk), [float(g) for g in dstate]


def _ret_kernel(x_ref, cos_ref, sin_ref, dmat_ref, dq_ref, dk_ref, g_ref, b_ref, o_ref, st_ref,
                *, dstate):
    @pl.when(pl.program_id(1) == 0)
    def _():
        st_ref[...] = jnp.zeros(st_ref.shape, F32)

    cos = cos_ref[...]
    sin = sin_ref[...]
    for h in range(RET_HEADS):
        c0 = h * HEAD_DIM
        q = x_ref[:, c0:c0 + HEAD_DIM].astype(F32)
        k = x_ref[:, RET_WIDTH + c0:RET_WIDTH + c0 + HEAD_DIM].astype(F32)
        v = x_ref[:, 2 * RET_WIDTH + c0:2 * RET_WIDTH + c0 + HEAD_DIM]
        gate = x_ref[:, 3 * RET_WIDTH + c0:3 * RET_WIDTH + c0 + HEAD_DIM].astype(F32)
        q = q * cos + pltpu.roll(q, HEAD_DIM // 2, 1) * sin
        k = k * cos + pltpu.roll(k, HEAD_DIM // 2, 1) * sin
        s = _dot_nt(q.astype(BF16), k.astype(BF16)) * dmat_ref[h]
        o = _dot(s.astype(BF16), v)
        state = st_ref[h]
        o = o + _dot((q * dq_ref[:, c0:c0 + HEAD_DIM]).astype(BF16), state.astype(BF16))
        kd = (k * dk_ref[:, c0:c0 + HEAD_DIM]).astype(BF16)
        st_ref[h] = state * dstate[h] + _dot_tn(kd, v)
        o = _layer_norm_rows(o) * g_ref[:, c0:c0 + HEAD_DIM] + b_ref[:, c0:c0 + HEAD_DIM]
        o_ref[:, c0:c0 + HEAD_DIM] = (gate * jax.nn.sigmoid(gate) * o).astype(o_ref.dtype)


def _ret(proj, gn_g, gn_b, batch, seq, col0, t):
    m = proj.shape[0]
    nt = seq // t
    cos2, sin2, dmat, dq, dk, dstate = _ret_tables(seq, t)
    cb = col0 // (4 * RET_WIDTH)
    return pl.pallas_call(
        functools.partial(_ret_kernel, dstate=dstate),
        grid=(batch, nt),
        in_specs=[pl.BlockSpec((t, 4 * RET_WIDTH), lambda b, i: (b * nt + i, cb)),
                  pl.BlockSpec((t, HEAD_DIM), lambda b, i: (i, 0)),
                  pl.BlockSpec((t, HEAD_DIM), lambda b, i: (i, 0)),
                  pl.BlockSpec((RET_HEADS, t, t), lambda b, i: (0, 0, 0)),
                  pl.BlockSpec((t, RET_WIDTH), lambda b, i: (0, 0)),
                  pl.BlockSpec((t, RET_WIDTH), lambda b, i: (0, 0)),
                  pl.BlockSpec((1, RET_WIDTH), lambda b, i: (0, 0)),
                  pl.BlockSpec((1, RET_WIDTH), lambda b, i: (0, 0))],
        out_specs=pl.BlockSpec((t, RET_WIDTH), lambda b, i: (b * nt + i, 0)),
        out_shape=jax.ShapeDtypeStruct((m, RET_WIDTH), BF16),
        scratch_shapes=[pltpu.VMEM((RET_HEADS, HEAD_DIM, HEAD_DIM), F32)],
        compiler_params=_params("parallel", "arbitrary"),
        name="ret",
    )(proj, cos2, sin2, dmat, dq, dk, gn_g.reshape(1, -1), gn_b.reshape(1, -1))


def _merge_kernel(of_ref, os_ref, or_ref, gl_ref, x_ref, g1_ref, wf_ref, ws_ref, wr_ref, wo_ref,
                  lg_ref, lb_ref, o_ref, *, d, alpha):
    def gate(n):
        return jax.nn.sigmoid(gl_ref[:, n * d:(n + 1) * d].astype(F32))

    merged = gate(0) * _dot(of_ref[...], wf_ref[...])
    merged = merged + gate(1) * _dot(os_ref[...], ws_ref[...])
    merged = merged + gate(2) * _dot(or_ref[...], wr_ref[...])
    mix = _dot(merged.astype(BF16), wo_ref[...])
    y = alpha * x_ref[...] + (1.0 + g1_ref[0]) * mix
    o_ref[...] = _layer_norm_rows(y) * lg_ref[...] + lb_ref[...]


def _merge(o_fox, o_sg, o_ret, proj, x2d, gate1, w_fox, w_sg, w_ret, w_out, ln_g, ln_b,
           seq, tm, alpha):
    m, d = x2d.shape
    per_batch = seq // tm
    return pl.pallas_call(
        functools.partial(_merge_kernel, d=d, alpha=alpha),
        grid=(m // tm,),
        in_specs=[pl.BlockSpec((tm, FOX_WIDTH), lambda i: (i, 0)),
                  pl.BlockSpec((tm, SG_WIDTH), lambda i: (i, 0)),
                  pl.BlockSpec((tm, RET_WIDTH), lambda i: (i, 0)),
                  pl.BlockSpec((tm, N_BRANCH * d), lambda i: (i, 0)),
                  pl.BlockSpec((tm, d), lambda i: (i, 0)),
                  pl.BlockSpec((1, 1, d), lambda i: (i // per_batch, 0, 0)),
                  _const_spec((FOX_WIDTH, d)),
                  _const_spec((SG_WIDTH, d)),
                  _const_spec((RET_WIDTH, d)),
                  _const_spec((d, d)),
                  _const_spec((1, d)),
                  _const_spec((1, d))],
        out_specs=pl.BlockSpec((tm, d), lambda i: (i, 0)),
        out_shape=jax.ShapeDtypeStruct((m, d), F32),
        compiler_params=_params("parallel"),
        name="merge",
    )(o_fox, o_sg, o_ret, proj, x2d, gate1, w_fox, w_sg, w_ret, w_out,
      ln_g.reshape(1, -1), ln_b.reshape(1, -1))


def _ffn_kernel(x_ref, sh_ref, sc_ref, g2_ref, wa_ref, wu_ref, wo_ref, lg_ref, lb_ref, o_ref,
                h_ref, acc_ref, *, alpha):
    j = pl.program_id(1)

    @pl.when(j == 0)
    def _():
        h = _layer_norm_rows(x_ref[...]) * (1.0 + sc_ref[0]) + sh_ref[0]
        h_ref[...] = h.astype(BF16)
        acc_ref[...] = jnp.zeros(acc_ref.shape, F32)

    h = h_ref[...]
    a = _dot(h, wa_ref[...])
    u = _dot(h, wu_ref[...])
    act = (a * jax.nn.sigmoid(a) * u).astype(BF16)
    acc_ref[...] += _dot(act, wo_ref[...])

    @pl.when(j == pl.num_programs(1) - 1)
    def _():
        y = alpha * x_ref[...] + (1.0 + g2_ref[0]) * acc_ref[...]
        o_ref[...] = _layer_norm_rows(y) * lg_ref[...] + lb_ref[...]


def _ffn(x2d, shift, scale, gate2, w_in, w_out, ln_g, ln_b, seq, tm, tf, alpha):
    m, d = x2d.shape
    d_ff = w_out.shape[0]
    nf = d_ff // tf
    per_batch = seq // tm
    mod_spec = pl.BlockSpec((1, 1, d), lambda i, j: (i // per_batch, 0, 0))
    return pl.pallas_call(
        functools.partial(_ffn_kernel, alpha=alpha),
        grid=(m // tm, nf),
        in_specs=[pl.BlockSpec((tm, d), lambda i, j: (i, 0)),
                  mod_spec, mod_spec, mod_spec,
                  pl.BlockSpec((d, tf), lambda i, j: (0, j)),
                  pl.BlockSpec((d, tf), lambda i, j: (0, nf + j)),
                  pl.BlockSpec((tf, d), lambda i, j: (j, 0)),
                  _const_spec((1, d)),
                  _const_spec((1, d))],
        out_specs=pl.BlockSpec((tm, d), lambda i, j: (i, 0)),
        out_shape=jax.ShapeDtypeStruct((m, d), F32),
        scratch_shapes=[pltpu.VMEM((tm, d), BF16), pltpu.VMEM((tm, d), F32)],
        compiler_params=_params("parallel", "arbitrary"),
        name="ffn",
    )(x2d, shift, scale, gate2, w_in, w_in, w_out, ln_g.reshape(1, -1), ln_b.reshape(1, -1))


def _tile(n, want):
    t = min(n, want)
    assert n % t == 0, (n, t)
    return t


def _layer(x2d, mod, batch, seq, depth, w_in, b_in, sg_ln_g, sg_ln_b, sg_w, sg_b, ret_gn_g,
           ret_gn_b, w_branch, w_out, ln1_g, ln1_b, w_ffn_in, w_ffn_out, ln2_g, ln2_b):
    d = x2d.shape[1]
    alpha = (2 * depth) ** 0.25
    shift1, scale1, gate1, shift2, scale2, gate2 = [
        mod[:, k * d:(k + 1) * d].reshape(batch, 1, d) for k in range(6)]

    sizes = (FOX_WIDTH, FOX_WIDTH, FOX_WIDTH, FOX_HEADS, 2 * SG_WIDTH,
             RET_WIDTH, RET_WIDTH, RET_WIDTH, RET_WIDTH, N_BRANCH * d)
    offs = np.concatenate([[0], np.cumsum(sizes)])
    seg = lambda a, k: a[..., offs[k]:offs[k + 1]]
    att_scale = HEAD_DIM ** -0.5
    order = [(9, 1.0), (5, att_scale), (6, 1.0), (7, 1.0), (8, 1.0), (4, 1.0),
             (0, att_scale), (1, 1.0), (2, 1.0)]
    w_main = jnp.concatenate([seg(w_in, k) * s for k, s in order], axis=1).astype(BF16)
    b_main = jnp.concatenate([seg(b_in, k) * s for k, s in order]).reshape(1, -1)
    col_gates, col_ret = 0, N_BRANCH * d
    col_sg = col_ret + 4 * RET_WIDTH
    col_fox = col_sg + 2 * SG_WIDTH
    w_f = jnp.zeros((d, LANES), F32).at[:, :FOX_HEADS].set(seg(w_in, 3))
    b_f = jnp.zeros((1, LANES), F32).at[0, :FOX_HEADS].set(seg(b_in, 3))

    n_main = w_main.shape[1]
    tn = 1792 if n_main % 1792 == 0 else 128
    proj, f_logit = _inproj(x2d, shift1, scale1, w_main, b_main, w_f, b_f, seq,
                            _tile(seq, 512), tn)

    f_rows = f_logit[:, :FOX_HEADS].reshape(batch, seq, FOX_HEADS).transpose(0, 2, 1)
    cf = _cumf(f_rows.reshape(batch * FOX_HEADS, seq // LANES, LANES))

    o_fox = _fox(proj, cf, batch, seq, col_fox, _tile(seq, 512), _tile(seq, 512))
    o_sg = _sg(proj, sg_ln_g, sg_ln_b, sg_w, sg_b, col_sg, _tile(seq, 512))
    o_ret = _ret(proj, ret_gn_g, ret_gn_b, batch, seq, col_ret, _tile(seq, 256))

    wb = w_branch.astype(BF16)
    x2d = _merge(o_fox, o_sg, o_ret, proj, x2d, gate1,
                 wb[:FOX_WIDTH], wb[FOX_WIDTH:FOX_WIDTH + SG_WIDTH], wb[FOX_WIDTH + SG_WIDTH:],
                 w_out.astype(BF16), ln1_g, ln1_b, seq, _tile(seq, 256), alpha)
    assert col_gates == 0
    x2d = _ffn(x2d, shift2, scale2, gate2, w_ffn_in.astype(BF16), w_ffn_out.astype(BF16),
               ln2_g, ln2_b, seq, _tile(seq, 512), 512, alpha)
    return x2d


def kernel(x, c, w_ada, b_ada, w_in, b_in, sg_ln_g, sg_ln_b, sg_w, sg_b, ret_gn_g, ret_gn_b,
           w_branch, w_out, ln1_g, ln1_b, w_ffn_in, w_ffn_out, ln2_g, ln2_b):
    batch, seq, d = x.shape
    depth = w_ada.shape[0]
    mod = _ada(c, w_ada, b_ada)
    x2d = x.reshape(batch * seq, d)
    for l in range(depth):
        x2d = _layer(x2d, mod[l], batch, seq, depth, w_in[l], b_in[l], sg_ln_g[l], sg_ln_b[l],
                     sg_w[l], sg_b[l], ret_gn_g[l], ret_gn_b[l], w_branch[l], w_out[l],
                     ln1_g[l], ln1_b[l], w_ffn_in[l], w_ffn_out[l], ln2_g[l], ln2_b[l])
    return x2d.reshape(batch, seq, d)
```
